```python
import jax, jax.numpy as jnp
from jax import lax
import numpy as np

D_MODEL = 2048
BATCH = 2
SEQ = 16384
DEPTH = 2

HG_HEADS = 16
HG_EXPAND = 128
HG_KEY_DIM = HG_HEADS * HG_EXPAND
HG_VAL_DIM = 2048
HG_HEAD_V = HG_VAL_DIM // HG_HEADS
CHUNK = 64
CONV_DIM = 2048
CONV_WIDTH = 3
FFN_DIM = 5632
FFN_CONV_WIDTH = 3
EPS = 1e-6
F_MIN = 1e-30

SPLITS = (HG_KEY_DIM, HG_KEY_DIM, HG_VAL_DIM, HG_VAL_DIM,
          CONV_DIM, CONV_DIM, CONV_DIM,
          D_MODEL, D_MODEL)
IN_DIM = sum(SPLITS)

kernel_name = "hgrn2_shortconv_gated_hybrid"


def rms_norm(x, w):
    xf = x.astype(jnp.float32)
    y = xf * lax.rsqrt(jnp.mean(xf * xf, axis=-1, keepdims=True) + EPS)
    return (y * w.astype(jnp.float32)).astype(x.dtype)


def causal_dwconv(x, w, b=None):
    k_width, ch = w.shape
    y = lax.conv_general_dilated(
        x, w[:, None, :].astype(x.dtype), window_strides=(1,), padding=((k_width - 1, 0),),
        dimension_numbers=('NWC', 'WIO', 'NWC'), feature_group_count=ch)
    if b is not None:
        y = y + b.astype(x.dtype)
    return y


def _to_chunks(t):
    bsz, seq, heads, d = t.shape
    return t.reshape(bsz, seq // CHUNK, CHUNK, heads, d).transpose(1, 0, 3, 2, 4)


def hgrn2_recurrence(q, k, v, log_f):
    bsz, seq, heads, dk = q.shape
    dv = v.shape[-1]
    xs = (_to_chunks(q), _to_chunks(k), _to_chunks(v), _to_chunks(log_f))
    mask = jnp.tril(jnp.ones((CHUNK, CHUNK), dtype=bool))[:, :, None]

    def step(state, inp):
        qc, kc, vc, gc = inp
        b = jnp.cumsum(gc, axis=2)
        o_inter = jnp.einsum('bhtk,bhkv->bhtv', qc * jnp.exp(b), state)
        diff = b[:, :, :, None, :] - b[:, :, None, :, :]
        decay = jnp.where(mask, jnp.exp(jnp.where(mask, diff, 0.0)), 0.0)
        scores = jnp.einsum('bhtk,bhsk,bhtsk->bhts', qc, kc, decay)
        o_intra = jnp.einsum('bhts,bhsv->bhtv', scores, vc)
        b_last = b[:, :, -1:, :]
        new_state = (jnp.exp(b_last[:, :, 0, :])[..., None] * state
                     + jnp.einsum('bhsk,bhsv->bhkv', kc * jnp.exp(b_last - b), vc))
        return new_state, o_inter + o_intra

    s0 = jnp.zeros((bsz, heads, dk, dv), jnp.float32)
    _, ys = lax.scan(step, s0, xs)
    return ys.transpose(1, 0, 3, 2, 4).reshape(bsz, seq, heads, dv)


def setup_inputs(seed: int = 0) -> dict:
    key = jax.random.key(seed)
    ks = jax.random.split(key, 17)
    f32 = jnp.float32

    def nrm(k, shape, scale):
        return jax.random.normal(k, shape, f32) * scale

    def gain(k, shape):
        return 1.0 + 0.02 * jax.random.normal(k, shape, f32)

    return {
        "x": jax.random.normal(ks[0], (BATCH, SEQ, D_MODEL), f32),
        "norm_mix_pre": gain(ks[1], (DEPTH, D_MODEL)),
        "norm_mix_post": gain(ks[2], (DEPTH, D_MODEL)),
        "norm_ffn_pre": gain(ks[3], (DEPTH, D_MODEL)),
        "norm_ffn_post": gain(ks[4], (DEPTH, D_MODEL)),
        "w_in": nrm(ks[5], (DEPTH, D_MODEL, IN_DIM), D_MODEL ** -0.5),
        "hg_lower_bounds": nrm(ks[6], (DEPTH, HG_KEY_DIM), 0.1),
        "hg_out_norm": gain(ks[7], (DEPTH, HG_HEAD_V)),
        "w_hg_out": nrm(ks[8], (DEPTH, HG_VAL_DIM, D_MODEL), HG_VAL_DIM ** -0.5),
        "sc_conv": nrm(ks[9], (DEPTH, CONV_WIDTH, CONV_DIM), CONV_WIDTH ** -0.5),
        "w_sc_out": nrm(ks[10], (DEPTH, CONV_DIM, D_MODEL), CONV_DIM ** -0.5),
        "w_mix_out": nrm(ks[11], (DEPTH, D_MODEL, D_MODEL), D_MODEL ** -0.5),
        "w_ffn_up": nrm(ks[12], (DEPTH, D_MODEL, 2 * FFN_DIM), D_MODEL ** -0.5),
        "ffn_conv": nrm(ks[13], (DEPTH, FFN_CONV_WIDTH, 2 * FFN_DIM), FFN_CONV_WIDTH ** -0.5),
        "ffn_conv_bias": nrm(ks[14], (DEPTH, 2 * FFN_DIM), 0.02),
        "w_ffn_down": nrm(ks[15], (DEPTH, FFN_DIM, D_MODEL), FFN_DIM ** -0.5),
    }


def reference(x, norm_mix_pre, norm_mix_post, norm_ffn_pre, norm_ffn_post, w_in,
              hg_lower_bounds, hg_out_norm, w_hg_out, sc_conv, w_sc_out, w_mix_out,
              w_ffn_up, ffn_conv, ffn_conv_bias, w_ffn_down):
    bsz, seq, _ = x.shape
    lb_soft = jax.nn.softmax(hg_lower_bounds.astype(jnp.float32), axis=0)
    lower = jnp.cumsum(lb_soft, axis=0) - lb_soft[0]
    offsets = []
    acc = 0
    for s in SPLITS[:-1]:
        acc += s
        offsets.append(acc)

    for l in range(DEPTH):
        h = rms_norm(x, norm_mix_pre[l])
        proj = h @ w_in[l]
        q_r, f_r, i_r, g_r, cb, cc, ch, ga, gb = jnp.split(proj, offsets, axis=-1)

        lb = lower[l]
        z = f_r.astype(jnp.float32)
        sig = jax.nn.sigmoid(z)
        f_gate = lb + (1.0 - lb) * sig
        log_f = jnp.log(jnp.maximum(f_gate, F_MIN))
        k_in = (1.0 - lb) * (1.0 - sig)
        q_in = jax.nn.silu(q_r.astype(jnp.float32))
        shp_k = (bsz, seq, HG_HEADS, HG_EXPAND)
        o = hgrn2_recurrence(q_in.reshape(shp_k), k_in.reshape(shp_k),
                             i_r.astype(jnp.float32).reshape(bsz, seq, HG_HEADS, HG_HEAD_V),
                             log_f.reshape(shp_k))
        o = rms_norm(o, hg_out_norm[l]).reshape(bsz, seq, HG_VAL_DIM).astype(x.dtype)
        y_a = (o * jax.nn.silu(g_r)) @ w_hg_out[l]

        u = causal_dwconv(cc * ch, sc_conv[l])
        y_b = (cb * u) @ w_sc_out[l]

        mixed = jax.nn.sigmoid(ga) * y_a + jax.nn.sigmoid(gb) * y_b
        x = x + rms_norm(mixed @ w_mix_out[l], norm_mix_post[l])

        h = rms_norm(x, norm_ffn_pre[l])
        up = causal_dwconv(h @ w_ffn_up[l], ffn_conv[l], ffn_conv_bias[l])
        a, b = jnp.split(up, [FFN_DIM], axis=-1)
        y = (jax.nn.silu(a) * b) @ w_ffn_down[l]
        x = x + rms_norm(y, norm_ffn_post[l])
    return x
```

```python
import functools

import jax
import jax.numpy as jnp
from jax import lax
from jax.experimental import pallas as pl
from jax.experimental.pallas import tpu as pltpu

HG_HEAD = 128
CHUNK = 64
N_PROJ = 9
EPS = 1e-6
F_MIN = 1e-30
CONV_W = 3

V7X_VMEM_BYTES = 64 * 1024 * 1024
VMEM_COMPILER_RESERVE = 6 * 1024 * 1024
SUBLANES = 8
BF16_ROWS = 16

F32 = jnp.float32
BF16 = jnp.bfloat16


def _vmem_limit(pipelined_bytes, resident_bytes=0, temp_bytes=0):
    need = 2 * pipelined_bytes + resident_bytes + temp_bytes + VMEM_COMPILER_RESERVE
    return int(min(need, V7X_VMEM_BYTES - VMEM_COMPILER_RESERVE))


def _nbytes(shape, dtype):
    n = 1
    for s in shape:
        n *= s
    return n * jnp.dtype(dtype).itemsize


def _dot(a, b):
    return jnp.dot(a, b, preferred_element_type=F32)


def _dot_nt(a, b):
    return lax.dot_general(a, b, (((1,), (1,)), ((), ())), preferred_element_type=F32)


def _dot_tn(a, b):
    return lax.dot_general(a, b, (((0,), (0,)), ((), ())), preferred_element_type=F32)


def _rms_scale(x):
    return x * lax.rsqrt(jnp.mean(x * x, axis=-1, keepdims=True) + EPS)


def _silu(x):
    return x * jax.nn.sigmoid(x)


def _resident(shape, index_map):
    return pl.BlockSpec(shape, index_map, pipeline_mode=pl.Buffered(1))


def _prenorm_kernel(x_ref, g_ref, h_ref):
    h_ref[...] = (_rms_scale(x_ref[...]) * g_ref[...]).astype(h_ref.dtype)


def _prenorm(x, gain, tm):
    n, d = x.shape
    return pl.pallas_call(
        _prenorm_kernel,
        grid=(n // tm,),
        in_specs=[pl.BlockSpec((tm, d), lambda i: (i, 0)),
                  pl.BlockSpec((1, d), lambda i: (0, 0))],
        out_specs=pl.BlockSpec((tm, d), lambda i: (i, 0)),
        out_shape=jax.ShapeDtypeStruct((n, d), BF16),
        compiler_params=pltpu.CompilerParams(
            dimension_semantics=("parallel",),
            vmem_limit_bytes=_vmem_limit(_nbytes((tm, d), F32) + _nbytes((tm, d), BF16),
                                         temp_bytes=_nbytes((tm, d), F32))),
        name="prenorm",
    )(x, gain.reshape(1, d))


def _inproj_kernel(layer, depth, h_ref, wq, wf, wi, wg, wcb, wcc, wch, wga, wgb, lbp_ref,
                   q_ref, k_ref, lf_ref, v_ref, gs_ref, cb_ref, p_ref, sa_ref, sb_ref):
    h = h_ref[...]

    rows = [lbp_ref[m:m + 1, :] for m in range(depth)]
    mx = functools.reduce(jnp.maximum, rows)
    ex = [jnp.exp(r - mx) for r in rows]
    tot = functools.reduce(lambda a, b: a + b, ex)
    lb = jnp.zeros_like(mx)
    for m in range(1, layer + 1):
        lb = lb + ex[m] / tot

    z = _dot(h, wf[...])
    sig = jax.nn.sigmoid(z)
    f_gate = lb + (1.0 - lb) * sig
    lf_ref[...] = jnp.log(jnp.maximum(f_gate, F_MIN))
    k_ref[...] = ((1.0 - lb) * (1.0 - sig)).astype(k_ref.dtype)
    q_ref[...] = _silu(_dot(h, wq[...])).astype(q_ref.dtype)
    v_ref[...] = _dot(h, wi[...]).astype(v_ref.dtype)
    gs_ref[...] = _silu(_dot(h, wg[...])).astype(gs_ref.dtype)
    cb_ref[...] = _dot(h, wcb[...]).astype(cb_ref.dtype)
    p_ref[...] = (_dot(h, wcc[...]) * _dot(h, wch[...])).astype(p_ref.dtype)
    sa_ref[...] = jax.nn.sigmoid(_dot(h, wga[...])).astype(sa_ref.dtype)
    sb_ref[...] = jax.nn.sigmoid(_dot(h, wgb[...])).astype(sb_ref.dtype)


def _inproj(h, w_in, lower_params, layer, tm, tc):
    n, d = h.shape
    width = w_in.shape[1] // N_PROJ
    depth = lower_params.shape[0]
    nj = width // tc
    w_specs = [_resident((d, tc), functools.partial(lambda g, j, i: (0, g * nj + j), g))
               for g in range(N_PROJ)]
    out_block = pl.BlockSpec((tm, tc), lambda j, i: (i, j))
    out_dtypes = [BF16, BF16, F32, BF16, BF16, BF16, BF16, BF16, BF16]
    pipelined = _nbytes((tm, d), BF16) + sum(_nbytes((tm, tc), t) for t in out_dtypes)
    return pl.pallas_call(
        functools.partial(_inproj_kernel, layer, depth),
        grid=(nj, n // tm),
        in_specs=[pl.BlockSpec((tm, d), lambda j, i: (i, 0))] + w_specs
                 + [pl.BlockSpec((depth, tc), lambda j, i: (0, j))],
        out_specs=[out_block] * N_PROJ,
        out_shape=[jax.ShapeDtypeStruct((n, width), t) for t in out_dtypes],
        compiler_params=pltpu.CompilerParams(
            dimension_semantics=("parallel", "parallel"),
            vmem_limit_bytes=_vmem_limit(pipelined, N_PROJ * _nbytes((d, tc), BF16),
                                         temp_bytes=4 * _nbytes((tm, tc), F32))),
        name="inproj",
    )(h, *([w_in] * N_PROJ), lower_params)


def _hgrn_kernel(n_heads, q_ref, k_ref, v_ref, lf_ref, gs_ref, gain_ref, o_ref, st_ref):
    tt = q_ref.shape[0]

    @pl.when(pl.program_id(2) == 0)
    def _():
        st_ref[...] = jnp.zeros_like(st_ref)

    row = lax.broadcasted_iota(jnp.int32, (CHUNK, CHUNK), 0)
    col = lax.broadcasted_iota(jnp.int32, (CHUNK, CHUNK), 1)
    causal = col <= row
    tri = causal.astype(BF16)
    gain = gain_ref[...]
    mid = CHUNK // 2 - 1

    def chunk(c, carry):
        r0 = pl.multiple_of(c * CHUNK, CHUNK)
        rows = pl.ds(r0, CHUNK)
        g = lf_ref[rows, :]
        g_hi = g.astype(BF16)
        rem = g - g_hi.astype(F32)
        g_mid = rem.astype(BF16)
        g_lo = (rem - g_mid.astype(F32)).astype(BF16)
        b = _dot(tri, g_hi) + _dot(tri, g_mid) + _dot(tri, g_lo)
        b_mid = b[mid:mid + 1, :]
        b_last = b[CHUNK - 1:CHUNK, :]
        e_q = jnp.exp(b - b_mid)
        e_k = jnp.exp(b_mid - b)
        e_mid = jnp.exp(b_mid)
        e_tail = jnp.exp(b_last - b_mid)
        q_t = q_ref[rows, :].astype(F32) * e_q
        k_t = k_ref[rows, :].astype(F32) * e_k
        q_in = (q_t * e_mid).astype(BF16)
        k_out = (k_t * e_tail).astype(BF16)
        q_t = q_t.astype(BF16)
        k_t = k_t.astype(BF16)
        d_last = e_mid * e_tail
        v = v_ref[rows, :]
        gs = gs_ref[rows, :].astype(F32)
        for h in range(n_heads):
            sl = slice(h * HG_HEAD, (h + 1) * HG_HEAD)
            st = st_ref[h]
            scores = jnp.where(causal, _dot_nt(q_t[:, sl], k_t[:, sl]), 0.0).astype(BF16)
            o = _dot(scores, v[:, sl]) + _dot_nt(q_in[:, sl], st.astype(BF16))
            st_ref[h] = st * d_last[:, sl] + _dot_tn(v[:, sl], k_out[:, sl])
            o = _rms_scale(o) * gain
            o_ref[rows, sl] = (o * gs[:, sl]).astype(o_ref.dtype)
        return carry

    lax.fori_loop(0, tt // CHUNK, chunk, 0)


def _hgrn(q, k, v, lf, gs, gain, seq, tt, n_heads):
    n, width = q.shape
    hb = n_heads * HG_HEAD
    tpb = seq // tt
    blk = pl.BlockSpec((tt, hb), lambda b, j, t: (b * tpb + t, j))
    pipelined = 4 * _nbytes((tt, hb), BF16) + _nbytes((tt, hb), F32) + _nbytes((tt, hb), BF16)
    return pl.pallas_call(
        functools.partial(_hgrn_kernel, n_heads),
        grid=(n // seq, width // hb, tpb),
        in_specs=[blk, blk, blk, blk, blk, pl.BlockSpec((1, HG_HEAD), lambda b, j, t: (0, 0))],
        out_specs=blk,
        out_shape=jax.ShapeDtypeStruct((n, width), BF16),
        scratch_shapes=[pltpu.VMEM((n_heads, HG_HEAD, HG_HEAD), F32)],
        compiler_params=pltpu.CompilerParams(
            dimension_semantics=("parallel", "parallel", "arbitrary"),
            vmem_limit_bytes=_vmem_limit(pipelined, _nbytes((n_heads, HG_HEAD, HG_HEAD), F32),
                                         temp_bytes=16 * _nbytes((CHUNK, hb), F32))),
        name="hgrn",
    )(q, k, v, lf, gs, gain.reshape(1, HG_HEAD))


def _causal_conv3(x, prev, w_ref):
    w0 = w_ref[0:1, :]
    w1 = w_ref[1:2, :]
    w2 = w_ref[2:3, :]
    body = w2 * x + w1 * pltpu.roll(x, 1, 0) + w0 * pltpu.roll(x, 2, 0)
    head = x[0:SUBLANES]
    ridx = lax.broadcasted_iota(jnp.int32, head.shape, 0)
    h1 = jnp.where(ridx < 1, pltpu.roll(prev, 1, 0), pltpu.roll(head, 1, 0))
    h2 = jnp.where(ridx < 2, pltpu.roll(prev, 2, 0), pltpu.roll(head, 2, 0))
    first = w2 * head + w1 * h1 + w0 * h2
    return jnp.concatenate([first, body[SUBLANES:]], axis=0)


def _merge_kernel(tiles_per_seq, oa_ref, cb_ref, p_ref, halo_ref, sa_ref, sb_ref, x_ref,
                  whg_ref, wsc_ref, wmix_ref, cw_ref, gpost_ref, gnext_ref, xo_ref, ho_ref):
    seq_start = (pl.program_id(0) % tiles_per_seq) == 0
    halo = halo_ref[BF16_ROWS - SUBLANES:BF16_ROWS, :].astype(F32)
    halo = jnp.where(seq_start, 0.0, halo)
    u = _causal_conv3(p_ref[...].astype(F32), halo, cw_ref)
    vb = (cb_ref[...].astype(F32) * u).astype(BF16)
    ya = _dot(oa_ref[...], whg_ref[...])
    yb = _dot(vb, wsc_ref[...])
    mixed = (sa_ref[...].astype(F32) * ya + sb_ref[...].astype(F32) * yb).astype(BF16)
    z = _dot(mixed, wmix_ref[...])
    xn = x_ref[...] + _rms_scale(z) * gpost_ref[...]
    xo_ref[...] = xn
    ho_ref[...] = (_rms_scale(xn) * gnext_ref[...]).astype(ho_ref.dtype)


def _merge(oa, cb, p, sa, sb, x, whg, wsc, wmix, conv_w, g_post, g_next, seq, tm):
    n, d = x.shape
    row = pl.BlockSpec((tm, d), lambda i: (i, 0))
    halo_blocks = tm // BF16_ROWS
    halo = pl.BlockSpec((BF16_ROWS, d), lambda i: (jnp.maximum(i * halo_blocks - 1, 0), 0))
    wspec = _resident((d, d), lambda i: (0, 0))
    vec = pl.BlockSpec((1, d), lambda i: (0, 0))
    pipelined = (5 * _nbytes((tm, d), BF16) + _nbytes((BF16_ROWS, d), BF16)
                 + 2 * _nbytes((tm, d), F32) + _nbytes((tm, d), BF16))
    return pl.pallas_call(
        functools.partial(_merge_kernel, seq // tm),
        grid=(n // tm,),
        in_specs=[row, row, row, halo, row, row, row, wspec, wspec, wspec,
                  pl.BlockSpec((CONV_W, d), lambda i: (0, 0)), vec, vec],
        out_specs=[row, row],
        out_shape=[jax.ShapeDtypeStruct((n, d), F32), jax.ShapeDtypeStruct((n, d), BF16)],
        compiler_params=pltpu.CompilerParams(
            dimension_semantics=("parallel",),
            vmem_limit_bytes=_vmem_limit(pipelined, 3 * _nbytes((d, d), BF16),
                                         temp_bytes=4 * _nbytes((tm, d), F32))),
        name="merge",
    )(oa, cb, p, p, sa, sb, x, whg, wsc, wmix, conv_w, g_post.reshape(1, d), g_next.reshape(1, d))


def _ffn_up_kernel(tiles_per_seq, h_ref, wa_ref, wb_ref, cwa_ref, cwb_ref, ba_ref, bb_ref,
                   act_ref, prev_a, prev_b):
    tm = h_ref.shape[0]
    seq_start = (pl.program_id(1) % tiles_per_seq) == 0
    h = h_ref[...]
    ua = _dot(h, wa_ref[...])
    ub = _dot(h, wb_ref[...])
    pa = jnp.where(seq_start, 0.0, prev_a[...])
    pb = jnp.where(seq_start, 0.0, prev_b[...])
    a = _causal_conv3(ua, pa, cwa_ref) + ba_ref[...]
    b = _causal_conv3(ub, pb, cwb_ref) + bb_ref[...]
    prev_a[...] = ua[tm - SUBLANES:tm]
    prev_b[...] = ub[tm - SUBLANES:tm]
    act_ref[...] = (_silu(a) * b).astype(act_ref.dtype)


def _ffn_up(h, w_up, conv_w, bias, seq, tm, tf):
    n, d = h.shape
    ffn = w_up.shape[1] // 2
    nj = ffn // tf
    wa = _resident((d, tf), lambda j, i: (0, j))
    wb = _resident((d, tf), lambda j, i: (0, nj + j))
    ca = pl.BlockSpec((CONV_W, tf), lambda j, i: (0, j))
    cb = pl.BlockSpec((CONV_W, tf), lambda j, i: (0, nj + j))
    ba = pl.BlockSpec((1, tf), lambda j, i: (0, j))
    bb = pl.BlockSpec((1, tf), lambda j, i: (0, nj + j))
    bias2 = bias.reshape(1, 2 * ffn)
    pipelined = _nbytes((tm, d), BF16) + _nbytes((tm, tf), BF16)
    return pl.pallas_call(
        functools.partial(_ffn_up_kernel, seq // tm),
        grid=(nj, n // tm),
        in_specs=[pl.BlockSpec((tm, d), lambda j, i: (i, 0)), wa, wb, ca, cb, ba, bb],
        out_specs=pl.BlockSpec((tm, tf), lambda j, i: (i, j)),
        out_shape=jax.ShapeDtypeStruct((n, ffn), BF16),
        scratch_shapes=[pltpu.VMEM((SUBLANES, tf), F32), pltpu.VMEM((SUBLANES, tf), F32)],
        compiler_params=pltpu.CompilerParams(
            dimension_semantics=("parallel", "arbitrary"),
            vmem_limit_bytes=_vmem_limit(pipelined, 2 * _nbytes((d, tf), BF16),
                                         temp_bytes=6 * _nbytes((tm, tf), F32))),
        name="ffn_up",
    )(h, w_up, w_up, conv_w, conv_w, bias2, bias2)


def _ffn_down_kernel(emit_next, act_ref, wd_ref, x_ref, gpost_ref, gnext_ref, xo_ref, *ho_ref):
    y = _dot(act_ref[...], wd_ref[...])
    xn = x_ref[...] + _rms_scale(y) * gpost_ref[...]
    xo_ref[...] = xn
    if emit_next:
        ho_ref[0][...] = (_rms_scale(xn) * gnext_ref[...]).astype(ho_ref[0].dtype)


def _ffn_down(act, wd, x, g_post, g_next, tm):
    n, d = x.shape
    ffn = act.shape[1]
    emit_next = g_next is not None
    row = pl.BlockSpec((tm, d), lambda i: (i, 0))
    vec = pl.BlockSpec((1, d), lambda i: (0, 0))
    out_specs = [row, row] if emit_next else [row]
    out_shape = [jax.ShapeDtypeStruct((n, d), F32)]
    if emit_next:
        out_shape.append(jax.ShapeDtypeStruct((n, d), BF16))
    g_next = g_post if g_next is None else g_next
    pipelined = _nbytes((tm, ffn), BF16) + 2 * _nbytes((tm, d), F32) + _nbytes((tm, d), BF16)
    return pl.pallas_call(
        functools.partial(_ffn_down_kernel, emit_next),
        grid=(n // tm,),
        in_specs=[pl.BlockSpec((tm, ffn), lambda i: (i, 0)), _resident((ffn, d), lambda i: (0, 0)),
                  row, vec, vec],
        out_specs=out_specs,
        out_shape=out_shape,
        compiler_params=pltpu.CompilerParams(
            dimension_semantics=("parallel",),
            vmem_limit_bytes=_vmem_limit(pipelined, _nbytes((ffn, d), BF16),
                                         temp_bytes=2 * _nbytes((tm, d), F32))),
        name="ffn_down",
    )(act, wd, x, g_post.reshape(1, d), g_next.reshape(1, d))


TM_NORM = 512
TM_PROJ = 1024
TC_PROJ = 256
TT_HGRN = 512
HEADS_PER_BLOCK = 4
TM_MERGE = 256
TM_UP = 1024
TF_UP = 512
TM_DOWN = 256


def kernel(x, norm_mix_pre, norm_mix_post, norm_ffn_pre, norm_ffn_post, w_in, hg_lower_bounds,
           hg_out_norm, w_hg_out, sc_conv, w_sc_out, w_mix_out, w_ffn_up, ffn_conv, ffn_conv_bias,
           w_ffn_down):
    bsz, seq, d = x.shape
    depth = w_in.shape[0]
    n = bsz * seq
    xf = x.reshape(n, d)
    lower_params = hg_lower_bounds.astype(F32)
    h = _prenorm(xf, norm_mix_pre[0], TM_NORM)
    for l in range(depth):
        q, k, lf, v, gs, cb, p, sa, sb = _inproj(h, w_in[l].astype(BF16), lower_params, l,
                                                 TM_PROJ, TC_PROJ)
        oa = _hgrn(q, k, v, lf, gs, hg_out_norm[l], seq, TT_HGRN, HEADS_PER_BLOCK)
        xf, h = _merge(oa, cb, p, sa, sb, xf, w_hg_out[l].astype(BF16), w_sc_out[l].astype(BF16),
                       w_mix_out[l].astype(BF16), sc_conv[l], norm_mix_post[l], norm_ffn_pre[l],
                       seq, TM_MERGE)
        act = _ffn_up(h, w_ffn_up[l].astype(BF16), ffn_conv[l], ffn_conv_bias[l], seq, TM_UP, TF_UP)
        g_next = norm_mix_pre[l + 1] if l + 1 < depth else None
        res = _ffn_down(act, w_ffn_down[l].astype(BF16), xf, norm_ffn_post[l], g_next, TM_DOWN)
        xf = res[0]
        if g_next is not None:
            h = res[1]
    return xf.reshape(bsz, seq, d)
```

```python
import functools

import jax
import jax.numpy as jnp
from jax import lax
from jax.experimental import pallas as pl
from jax.experimental.pallas import tpu as pltpu

HG_HEAD = 128
CHUNK = 64
N_PROJ = 9
EPS = 1e-6
F_MIN = 1e-30
CONV_W = 3

V7X_VMEM_BYTES = 64 * 1024 * 1024
VMEM_COMPILER_RESERVE = 6 * 1024 * 1024
SUBLANES = 8
BF16_ROWS = 16

F32 = jnp.float32
BF16 = jnp.bfloat16


def _vmem_limit(pipelined_bytes, resident_bytes=0, temp_bytes=0):
    need = 2 * pipelined_bytes + resident_bytes + temp_bytes + VMEM_COMPILER_RESERVE
    return int(min(need, V7X_VMEM_BYTES - VMEM_COMPILER_RESERVE))


def _nbytes(shape, dtype):
    n = 1
    for s in shape:
        n *= s
    return n * jnp.dtype(dtype).itemsize


def _dot(a, b):
    return jnp.dot(a, b, preferred_element_type=F32)


def _dot_nt(a, b):
    return lax.dot_general(a, b, (((1,), (1,)), ((), ())), preferred_element_type=F32)


def _dot_tn(a, b):
    return lax.dot_general(a, b, (((0,), (0,)), ((), ())), preferred_element_type=F32)


def _rms_scale(x):
    return x * lax.rsqrt(jnp.mean(x * x, axis=-1, keepdims=True) + EPS)


def _silu(x):
    return x * jax.nn.sigmoid(x)


def _resident(shape, index_map):
    return pl.BlockSpec(shape, index_map, pipeline_mode=pl.Buffered(1))


def _prenorm_kernel(x_ref, g_ref, h_ref):
    h_ref[...] = (_rms_scale(x_ref[...]) * g_ref[...]).astype(h_ref.dtype)


def _prenorm(x, gain, tm):
    n, d = x.shape
    return pl.pallas_call(
        _prenorm_kernel,
        grid=(n // tm,),
        in_specs=[pl.BlockSpec((tm, d), lambda i: (i, 0)),
                  pl.BlockSpec((1, d), lambda i: (0, 0))],
        out_specs=pl.BlockSpec((tm, d), lambda i: (i, 0)),
        out_shape=jax.ShapeDtypeStruct((n, d), BF16),
        compiler_params=pltpu.CompilerParams(
            dimension_semantics=("parallel",),
            vmem_limit_bytes=_vmem_limit(_nbytes((tm, d), F32) + _nbytes((tm, d), BF16),
                                         temp_bytes=_nbytes((tm, d), F32))),
        name="prenorm",
    )(x, gain.reshape(1, d))


def _inproj_kernel(layer, depth, h_ref, wq, wf, wi, wg, wcb, wcc, wch, wga, wgb, lbp_ref,
                   q_ref, k_ref, lf_ref, v_ref, gs_ref, cb_ref, p_ref, sa_ref, sb_ref):
    h = h_ref[...]

    rows = [lbp_ref[m:m + 1, :] for m in range(depth)]
    mx = functools.reduce(jnp.maximum, rows)
    ex = [jnp.exp(r - mx) for r in rows]
    tot = functools.reduce(lambda a, b: a + b, ex)
    lb = jnp.zeros_like(mx)
    for m in range(1, layer + 1):
        lb = lb + ex[m] / tot

    z = _dot(h, wf[...])
    sig = jax.nn.sigmoid(z)
    f_gate = lb + (1.0 - lb) * sig
    lf_ref[...] = jnp.log(jnp.maximum(f_gate, F_MIN))
    k_ref[...] = ((1.0 - lb) * (1.0 - sig)).astype(k_ref.dtype)
    q_ref[...] = _silu(_dot(h, wq[...])).astype(q_ref.dtype)
    gs_ref[...] = _silu(_dot(h, wg[...])).astype(gs_ref.dtype)
    sa_ref[...] = jax.nn.sigmoid(_dot(h, wga[...])).astype(sa_ref.dtype)
    sb_ref[...] = jax.nn.sigmoid(_dot(h, wgb[...])).astype(sb_ref.dtype)
    p_ref[...] = (_dot(h, wcc[...]) * _dot(h, wch[...])).astype(p_ref.dtype)
    cb_ref[...] = _dot(h, wcb[...]).astype(cb_ref.dtype)
    v_ref[...] = _dot(h, wi[...]).astype(v_ref.dtype)


def _inproj(h, w_in, lower_params, layer, tm, tc):
    n, d = h.shape
    width = w_in.shape[2] // N_PROJ
    depth = lower_params.shape[0]
    nj = width // tc
    w_specs = [_resident((None, d, tc), functools.partial(lambda g, j, i: (layer, 0, g * nj + j), g))
               for g in range(N_PROJ)]
    out_block = pl.BlockSpec((tm, tc), lambda j, i: (i, j))
    out_dtypes = [BF16, BF16, F32, BF16, BF16, BF16, BF16, BF16, BF16]
    pipelined = _nbytes((tm, d), BF16) + sum(_nbytes((tm, tc), t) for t in out_dtypes)
    return pl.pallas_call(
        functools.partial(_inproj_kernel, layer, depth),
        grid=(nj, n // tm),
        in_specs=[pl.BlockSpec((tm, d), lambda j, i: (i, 0))] + w_specs
                 + [pl.BlockSpec((depth, tc), lambda j, i: (0, j))],
        out_specs=[out_block] * N_PROJ,
        out_shape=[jax.ShapeDtypeStruct((n, width), t) for t in out_dtypes],
        compiler_params=pltpu.CompilerParams(
            dimension_semantics=("parallel", "parallel"),
            vmem_limit_bytes=_vmem_limit(pipelined, N_PROJ * _nbytes((d, tc), BF16),
                                         temp_bytes=4 * _nbytes((tm, tc), F32))),
        name="inproj",
    )(h, *([w_in] * N_PROJ), lower_params)


def _hgrn_kernel(n_heads, q_ref, k_ref, v_ref, lf_ref, gs_ref, gain_ref, o_ref,
                 st_ref, b_s, qt_s, kt_s, qi_s, ko_s, dl_s, sc_s, kv_s, o_s):
    n_chunks = q_ref.shape[0] // CHUNK

    @pl.when(pl.program_id(2) == 0)
    def _():
        st_ref[...] = jnp.zeros_like(st_ref)

    row = lax.broadcasted_iota(jnp.int32, (CHUNK, CHUNK), 0)
    col = lax.broadcasted_iota(jnp.int32, (CHUNK, CHUNK), 1)
    causal = col <= row
    tri = causal.astype(BF16)
    gain = gain_ref[...]
    mid = CHUNK // 2 - 1
    chunk_rows = [slice(c * CHUNK, (c + 1) * CHUNK) for c in range(n_chunks)]
    head_cols = [slice(h * HG_HEAD, (h + 1) * HG_HEAD) for h in range(n_heads)]

    for rows in chunk_rows:
        g = lf_ref[rows, :]
        g_hi = g.astype(BF16)
        rem = g - g_hi.astype(F32)
        g_mid = rem.astype(BF16)
        g_lo = (rem - g_mid.astype(F32)).astype(BF16)
        b_s[rows, :] = _dot(tri, g_hi) + _dot(tri, g_mid) + _dot(tri, g_lo)

    for c, rows in enumerate(chunk_rows):
        b = b_s[rows, :]
        b_mid = b[mid:mid + 1, :]
        b_last = b[CHUNK - 1:CHUNK, :]
        e_mid = jnp.exp(b_mid)
        e_tail = jnp.exp(b_last - b_mid)
        q_t = q_ref[rows, :].astype(F32) * jnp.exp(b - b_mid)
        k_t = k_ref[rows, :].astype(F32) * jnp.exp(b_mid - b)
        qi_s[rows, :] = (q_t * e_mid).astype(BF16)
        ko_s[rows, :] = (k_t * e_tail).astype(BF16)
        qt_s[rows, :] = q_t.astype(BF16)
        kt_s[rows, :] = k_t.astype(BF16)
        dl_s[c:c + 1, :] = e_mid * e_tail

    for c, rows in enumerate(chunk_rows):
        for h, sl in enumerate(head_cols):
            scores = jnp.where(causal, _dot_nt(qt_s[rows, sl], kt_s[rows, sl]), 0.0)
            sc_s[rows, h * HG_HEAD:h * HG_HEAD + CHUNK] = scores.astype(BF16)
            kv_s[c, h] = _dot_tn(v_ref[rows, sl], ko_s[rows, sl])

    for c, rows in enumerate(chunk_rows):
        for h, sl in enumerate(head_cols):
            st = st_ref[h]
            o = (_dot(sc_s[rows, h * HG_HEAD:h * HG_HEAD + CHUNK], v_ref[rows, sl])
                 + _dot_nt(qi_s[rows, sl], st.astype(BF16)))
            st_ref[h] = st * dl_s[c:c + 1, sl] + kv_s[c, h]
            o_s[rows, sl] = o

    for sl in head_cols:
        o = _rms_scale(o_s[:, sl]) * gain
        o_ref[:, sl] = (o * gs_ref[:, sl].astype(F32)).astype(o_ref.dtype)


def _hgrn(q, k, v, lf, gs, gain, seq, tt, n_heads):
    n, width = q.shape
    hb = n_heads * HG_HEAD
    tpb = seq // tt
    n_chunks = tt // CHUNK
    blk = pl.BlockSpec((tt, hb), lambda b, j, t: (b * tpb + t, j))
    pipelined = 4 * _nbytes((tt, hb), BF16) + _nbytes((tt, hb), F32) + _nbytes((tt, hb), BF16)
    scratch = [((n_heads, HG_HEAD, HG_HEAD), F32),
               ((tt, hb), F32),
               ((tt, hb), BF16), ((tt, hb), BF16),
               ((tt, hb), BF16), ((tt, hb), BF16),
               ((n_chunks, hb), F32),
               ((tt, hb), BF16),
               ((n_chunks, n_heads, HG_HEAD, HG_HEAD), F32),
               ((tt, hb), F32)]
    return pl.pallas_call(
        functools.partial(_hgrn_kernel, n_heads),
        grid=(n // seq, width // hb, tpb),
        in_specs=[blk, blk, blk, blk, blk, pl.BlockSpec((1, HG_HEAD), lambda b, j, t: (0, 0))],
        out_specs=blk,
        out_shape=jax.ShapeDtypeStruct((n, width), BF16),
        scratch_shapes=[pltpu.VMEM(s, t) for s, t in scratch],
        compiler_params=pltpu.CompilerParams(
            dimension_semantics=("parallel", "parallel", "arbitrary"),
            vmem_limit_bytes=_vmem_limit(pipelined, sum(_nbytes(s, t) for s, t in scratch),
                                         temp_bytes=16 * _nbytes((CHUNK, hb), F32))),
        name="hgrn",
    )(q, k, v, lf, gs, gain.reshape(1, HG_HEAD))


def _causal_conv3(x, prev, w):
    w0 = w[0:1, :]
    w1 = w[1:2, :]
    w2 = w[2:3, :]
    body = w2 * x + w1 * pltpu.roll(x, 1, 0) + w0 * pltpu.roll(x, 2, 0)
    head = x[0:SUBLANES]
    ridx = lax.broadcasted_iota(jnp.int32, head.shape, 0)
    h1 = jnp.where(ridx < 1, pltpu.roll(prev, 1, 0), pltpu.roll(head, 1, 0))
    h2 = jnp.where(ridx < 2, pltpu.roll(prev, 2, 0), pltpu.roll(head, 2, 0))
    first = w2 * head + w1 * h1 + w0 * h2
    return jnp.concatenate([first, body[SUBLANES:]], axis=0)


def _branch_kernel(tiles_per_seq, nb, oa_ref, cb_ref, p_ref, halo_ref, sa_ref, sb_ref,
                   whg_ref, wsc_ref, cw_ref, mixed_ref, vb_s):
    d = mixed_ref.shape[1]
    seq_start = (pl.program_id(0) % tiles_per_seq) == 0
    halo = halo_ref[BF16_ROWS - SUBLANES:BF16_ROWS, :].astype(F32)
    halo = jnp.where(seq_start, 0.0, halo)
    u = _causal_conv3(p_ref[...].astype(F32), halo, cw_ref[...])
    vb_s[...] = (cb_ref[...].astype(F32) * u).astype(BF16)
    col_blocks = [slice(c, c + nb) for c in range(0, d, nb)]
    oa = oa_ref[...]
    ya = [_dot(oa, whg_ref[:, cols]) for cols in col_blocks]
    vb = vb_s[...]
    nxt = _dot(vb, wsc_ref[:, col_blocks[0]])
    for s, (cols, ya_blk) in enumerate(zip(col_blocks, ya)):
        yb_blk = nxt
        if s + 1 < len(col_blocks):
            nxt = _dot(vb, wsc_ref[:, col_blocks[s + 1]])
        mixed_ref[:, cols] = (sa_ref[:, cols].astype(F32) * ya_blk
                              + sb_ref[:, cols].astype(F32) * yb_blk).astype(mixed_ref.dtype)


def _branch(oa, cb, p, sa, sb, whg, wsc, conv_w, layer, seq, tm, nb):
    n, d = oa.shape
    row = pl.BlockSpec((tm, d), lambda i: (i, 0))
    halo_blocks = tm // BF16_ROWS
    halo = pl.BlockSpec((BF16_ROWS, d), lambda i: (jnp.maximum(i * halo_blocks - 1, 0), 0))
    wspec = _resident((None, d, d), lambda i: (layer, 0, 0))
    pipelined = 6 * _nbytes((tm, d), BF16) + _nbytes((BF16_ROWS, d), BF16)
    return pl.pallas_call(
        functools.partial(_branch_kernel, seq // tm, nb),
        grid=(n // tm,),
        in_specs=[row, row, row, halo, row, row, wspec, wspec,
                  pl.BlockSpec((None, CONV_W, d), lambda i: (layer, 0, 0))],
        out_specs=row,
        out_shape=jax.ShapeDtypeStruct((n, d), BF16),
        scratch_shapes=[pltpu.VMEM((tm, d), BF16)],
        compiler_params=pltpu.CompilerParams(
            dimension_semantics=("parallel",),
            vmem_limit_bytes=_vmem_limit(pipelined, 2 * _nbytes((d, d), BF16) + _nbytes((tm, d), BF16),
                                         temp_bytes=3 * _nbytes((tm, d), F32))),
        name="branch",
    )(oa, cb, p, p, sa, sb, whg, wsc, conv_w)


def _mixout_kernel(rb, mixed_ref, x_ref, wmix_ref, gpost_ref, gnext_ref, xo_ref, ho_ref):
    tm = x_ref.shape[0]
    row_blocks = [slice(r, r + rb) for r in range(0, tm, rb)]
    nxt = _dot(mixed_ref[row_blocks[0], :], wmix_ref[...])
    for s, rows in enumerate(row_blocks):
        z = nxt
        if s + 1 < len(row_blocks):
            nxt = _dot(mixed_ref[row_blocks[s + 1], :], wmix_ref[...])
        xn = x_ref[rows, :] + _rms_scale(z) * gpost_ref[...]
        xo_ref[rows, :] = xn
        ho_ref[rows, :] = (_rms_scale(xn) * gnext_ref[...]).astype(ho_ref.dtype)


def _mixout(mixed, x, wmix, g_post, g_next, layer, tm, rb):
    n, d = x.shape
    row = pl.BlockSpec((tm, d), lambda i: (i, 0))
    vec = pl.BlockSpec((1, d), lambda i: (0, 0))
    pipelined = 2 * _nbytes((tm, d), BF16) + 2 * _nbytes((tm, d), F32)
    return pl.pallas_call(
        functools.partial(_mixout_kernel, rb),
        grid=(n // tm,),
        in_specs=[row, row, _resident((None, d, d), lambda i: (layer, 0, 0)), vec, vec],
        out_specs=[row, row],
        out_shape=[jax.ShapeDtypeStruct((n, d), F32), jax.ShapeDtypeStruct((n, d), BF16)],
        compiler_params=pltpu.CompilerParams(
            dimension_semantics=("parallel",),
            vmem_limit_bytes=_vmem_limit(pipelined, _nbytes((d, d), BF16),
                                         temp_bytes=3 * _nbytes((rb, d), F32))),
        name="mixout",
    )(mixed, x, wmix, g_post.reshape(1, d), g_next.reshape(1, d))


def _ffn_up_kernel(tiles_per_seq, h_ref, wa_ref, wb_ref, cwa_ref, cwb_ref, ba_ref, bb_ref,
                   act_ref, prev_a, prev_b):
    tm = h_ref.shape[0]
    seq_start = (pl.program_id(1) % tiles_per_seq) == 0
    h = h_ref[...]
    ua = _dot(h, wa_ref[...])
    ub = _dot(h, wb_ref[...])
    pa = jnp.where(seq_start, 0.0, prev_a[...])
    pb = jnp.where(seq_start, 0.0, prev_b[...])
    a = _causal_conv3(ua, pa, cwa_ref[...]) + ba_ref[...]
    b = _causal_conv3(ub, pb, cwb_ref[...]) + bb_ref[...]
    prev_a[...] = ua[tm - SUBLANES:tm]
    prev_b[...] = ub[tm - SUBLANES:tm]
    act_ref[...] = (_silu(a) * b).astype(act_ref.dtype)


def _ffn_up(h, w_up, conv_w, bias, layer, seq, tm, tf):
    n, d = h.shape
    ffn = w_up.shape[2] // 2
    nj = ffn // tf
    wa = _resident((None, d, tf), lambda j, i: (layer, 0, j))
    wb = _resident((None, d, tf), lambda j, i: (layer, 0, nj + j))
    ca = pl.BlockSpec((None, CONV_W, tf), lambda j, i: (layer, 0, j))
    cb = pl.BlockSpec((None, CONV_W, tf), lambda j, i: (layer, 0, nj + j))
    ba = pl.BlockSpec((None, 1, tf), lambda j, i: (layer, 0, j))
    bb = pl.BlockSpec((None, 1, tf), lambda j, i: (layer, 0, nj + j))
    pipelined = _nbytes((tm, d), BF16) + _nbytes((tm, tf), BF16)
    return pl.pallas_call(
        functools.partial(_ffn_up_kernel, seq // tm),
        grid=(nj, n // tm),
        in_specs=[pl.BlockSpec((tm, d), lambda j, i: (i, 0)), wa, wb, ca, cb, ba, bb],
        out_specs=pl.BlockSpec((tm, tf), lambda j, i: (i, j)),
        out_shape=jax.ShapeDtypeStruct((n, ffn), BF16),
        scratch_shapes=[pltpu.VMEM((SUBLANES, tf), F32), pltpu.VMEM((SUBLANES, tf), F32)],
        compiler_params=pltpu.CompilerParams(
            dimension_semantics=("parallel", "arbitrary"),
            vmem_limit_bytes=_vmem_limit(pipelined, 2 * _nbytes((d, tf), BF16),
                                         temp_bytes=6 * _nbytes((tm, tf), F32))),
        name="ffn_up",
    )(h, w_up, w_up, conv_w, conv_w, bias, bias)


def _ffn_down_kernel(emit_next, act_ref, wd_ref, x_ref, gpost_ref, gnext_ref, xo_ref, *ho_ref):
    y = _dot(act_ref[...], wd_ref[...])
    xn = x_ref[...] + _rms_scale(y) * gpost_ref[...]
    xo_ref[...] = xn
    if emit_next:
        ho_ref[0][...] = (_rms_scale(xn) * gnext_ref[...]).astype(ho_ref[0].dtype)


def _ffn_down(act, wd, x, g_post, g_next, layer, tm):
    n, d = x.shape
    ffn = act.shape[1]
    emit_next = g_next is not None
    row = pl.BlockSpec((tm, d), lambda i: (i, 0))
    vec = pl.BlockSpec((1, d), lambda i: (0, 0))
    out_specs = [row, row] if emit_next else [row]
    out_shape = [jax.ShapeDtypeStruct((n, d), F32)]
    if emit_next:
        out_shape.append(jax.ShapeDtypeStruct((n, d), BF16))
    g_next = g_post if g_next is None else g_next
    pipelined = _nbytes((tm, ffn), BF16) + 2 * _nbytes((tm, d), F32) + _nbytes((tm, d), BF16)
    return pl.pallas_call(
        functools.partial(_ffn_down_kernel, emit_next),
        grid=(n // tm,),
        in_specs=[pl.BlockSpec((tm, ffn), lambda i: (i, 0)),
                  _resident((None, ffn, d), lambda i: (layer, 0, 0)), row, vec, vec],
        out_specs=out_specs,
        out_shape=out_shape,
        compiler_params=pltpu.CompilerParams(
            dimension_semantics=("parallel",),
            vmem_limit_bytes=_vmem_limit(pipelined, _nbytes((ffn, d), BF16),
                                         temp_bytes=2 * _nbytes((tm, d), F32))),
        name="ffn_down",
    )(act, wd, x, g_post.reshape(1, d), g_next.reshape(1, d))


TM_NORM = 512
TM_PROJ = 1024
TC_PROJ = 512
TT_HGRN = 512
HEADS_PER_BLOCK = 4
TM_BRANCH = 512
NB_BRANCH = 512
TM_MIX = 512
RB_MIX = 256
TM_UP = 1024
TF_UP = 512
TM_DOWN = 256


def kernel(x, norm_mix_pre, norm_mix_post, norm_ffn_pre, norm_ffn_post, w_in, hg_lower_bounds,
           hg_out_norm, w_hg_out, sc_conv, w_sc_out, w_mix_out, w_ffn_up, ffn_conv, ffn_conv_bias,
           w_ffn_down):
    bsz, seq, d = x.shape
    depth = w_in.shape[0]
    n = bsz * seq
    xf = x.reshape(n, d)
    lower_params = hg_lower_bounds.astype(F32)
    w_in, w_hg_out, w_sc_out, w_mix_out, w_ffn_up, w_ffn_down = (
        w.astype(BF16) for w in (w_in, w_hg_out, w_sc_out, w_mix_out, w_ffn_up, w_ffn_down))
    ffn_bias = ffn_conv_bias.reshape(depth, 1, ffn_conv_bias.shape[1])
    h = _prenorm(xf, norm_mix_pre[0], TM_NORM)
    for l in range(depth):
        q, k, lf, v, gs, cb, p, sa, sb = _inproj(h, w_in, lower_params, l, TM_PROJ, TC_PROJ)
        oa = _hgrn(q, k, v, lf, gs, hg_out_norm[l], seq, TT_HGRN, HEADS_PER_BLOCK)
        mixed = _branch(oa, cb, p, sa, sb, w_hg_out, w_sc_out, sc_conv, l, seq, TM_BRANCH, NB_BRANCH)
        xf, h = _mixout(mixed, xf, w_mix_out, norm_mix_post[l], norm_ffn_pre[l], l, TM_MIX, RB_MIX)
        act = _ffn_up(h, w_ffn_up, ffn_conv, ffn_bias, l, seq, TM_UP, TF_UP)
        g_next = norm_mix_pre[l + 1] if l + 1 < depth else None
        res = _ffn_down(act, w_ffn_down, xf, norm_ffn_post[l], g_next, l, TM_DOWN)
        xf = res[0]
        if g_next is not None:
            h = res[1]
    return xf.reshape(bsz, seq, d)
```
